```python
import math
import jax, jax.numpy as jnp
from jax import lax
import numpy as np

D_MODEL = 1024
BATCH = 8
SEQ = 2048
DEPTH = 2

GRID_W = 64
CTX_LEN = 256

N_BRANCH = 4
BRANCH_W = 512
GDN_HEADS = 4
GDN_DK = 128
GDN_DV = 128
GDN_CHUNK = 64
SHORT_CONV = 5
CONF_KW = 31
CONF_GROUPS = 4
FFT_GROUPS = 4
SGU_CHUNK = 128
SGU_GROUPS = 4
MOE_GROUPS = 4
MOE_PER_GROUP = 4
N_EXPERTS = MOE_GROUPS * MOE_PER_GROUP
MOE_TOPK = 2
EXPERT_HIDDEN = 512

DN_ALPHA = (2 * DEPTH) ** 0.25
DN_BETA = (8 * DEPTH) ** -0.25
LN_EPS = 1e-6

OFF_Q = 0
OFF_K = OFF_Q + GDN_HEADS * GDN_DK
OFF_V = OFF_K + GDN_HEADS * GDN_DK
OFF_Z = OFF_V + GDN_HEADS * GDN_DV
OFF_A = OFF_Z + GDN_HEADS * GDN_DV
OFF_B = OFF_A + 2 * GDN_HEADS
OFF_CONF = OFF_B + 2 * GDN_HEADS
OFF_FFT = OFF_CONF + 2 * BRANCH_W
OFF_SGU = OFF_FFT + BRANCH_W
OFF_GATE = OFF_SGU + 2 * BRANCH_W
N_IN = OFF_GATE + N_BRANCH * D_MODEL

kernel_name = 'hybrid_gdn_conformer_fnet_sgu_hmoe'


def layer_norm(x, eps=LN_EPS):
    xf = x.astype(jnp.float32)
    mu = jnp.mean(xf, -1, keepdims=True)
    var = jnp.mean(jnp.square(xf - mu), -1, keepdims=True)
    return ((xf - mu) * lax.rsqrt(var + eps)).astype(x.dtype)


def affine_ln(x, g, b):
    return layer_norm(x) * g + b


def group_ln(x, g, b, groups):
    shp = x.shape
    y = layer_norm(x.reshape(*shp[:-1], groups, shp[-1] // groups)).reshape(shp)
    return y * g + b


def rms_norm(x, w, eps=LN_EPS):
    xf = x.astype(jnp.float32)
    return xf * lax.rsqrt(jnp.mean(jnp.square(xf), -1, keepdims=True) + eps) * w


def l2_normalize(x, eps=LN_EPS):
    xf = x.astype(jnp.float32)
    return xf * lax.rsqrt(jnp.sum(jnp.square(xf), -1, keepdims=True) + eps)


def modulate(x, shift, scale):
    return layer_norm(x) * (1 + scale) + shift


def depthwise_conv(x, w):
    k, ch = w.shape
    return lax.conv_general_dilated(x, w[:, None, :].astype(x.dtype), window_strides=(1,),
                                    padding=[(k // 2, k // 2)],
                                    dimension_numbers=('NWC', 'WIO', 'NWC'),
                                    feature_group_count=ch)


def sincos_2d(n, d):
    rows = n // GRID_W
    row = jnp.broadcast_to(jnp.arange(rows, dtype=jnp.float32)[:, None], (rows, GRID_W)).reshape(-1)
    col = jnp.broadcast_to(jnp.arange(GRID_W, dtype=jnp.float32)[None, :], (rows, GRID_W)).reshape(-1)
    quarter = d // 4
    omega = 1.0 / (10000.0 ** (jnp.arange(quarter, dtype=jnp.float32) / quarter))

    def enc(pos):
        ang = pos[:, None] * omega[None, :]
        return jnp.concatenate([jnp.sin(ang), jnp.cos(ang)], -1)

    return jnp.concatenate([enc(row), enc(col)], -1)


def gdn_chunk_scan(q, k, v, g, beta, s0):
    bsz, t, h, dk = q.shape
    dv = v.shape[-1]
    cs = GDN_CHUNK
    n = t // cs

    def to_chunks(a):
        a = a.reshape(bsz, n, cs, h, *a.shape[3:])
        return jnp.moveaxis(a, 3, 1)

    q, k, v, g, beta = (to_chunks(a) for a in (q, k, v, g, beta))
    gc = jnp.cumsum(g, axis=-1)
    incl = jnp.tril(jnp.ones((cs, cs), bool))
    strict = jnp.tril(jnp.ones((cs, cs), bool), -1)
    decay = jnp.exp(jnp.where(incl, gc[..., :, None] - gc[..., None, :], -jnp.inf))
    kb = k * beta[..., None]
    kk = jnp.einsum('bhnid,bhnjd->bhnij', kb, k) * decay
    a_mat = jnp.eye(cs, dtype=jnp.float32) + jnp.where(strict, kk, 0.0)
    rhs = jnp.concatenate([v * beta[..., None], kb * jnp.exp(gc)[..., None]], -1)
    sol = lax.linalg.triangular_solve(a_mat, rhs, left_side=True, lower=True, unit_diagonal=True)
    u, w = sol[..., :dv], sol[..., dv:]
    attn = jnp.einsum('bhnid,bhnjd->bhnij', q, k) * decay
    qg = q * jnp.exp(gc)[..., None]
    g_last = gc[..., -1]
    kd = k * jnp.exp(g_last[..., None] - gc)[..., None]

    def step(s, xs):
        w_n, u_n, qg_n, kd_n, attn_n, gl_n = xs
        v_new = u_n - jnp.einsum('bhcd,bhde->bhce', w_n, s)
        o_n = jnp.einsum('bhcd,bhde->bhce', qg_n, s) + jnp.einsum('bhij,bhje->bhie', attn_n, v_new)
        s = s * jnp.exp(gl_n)[..., None, None] + jnp.einsum('bhcd,bhce->bhde', kd_n, v_new)
        return s, o_n

    xs = tuple(jnp.moveaxis(a, 2, 0) for a in (w, u, qg, kd, attn, g_last))
    s_fin, o = lax.scan(step, s0, xs)
    o = o.transpose(1, 0, 3, 2, 4).reshape(bsz, t, h, dv)
    return o, s_fin


def gdn_inputs(p, conv_w, a_log, dt_bias):
    bsz, t, _ = p.shape
    qkv = jax.nn.silu(depthwise_conv(p[..., OFF_Q:OFF_Z], conv_w)).astype(jnp.float32)
    q = l2_normalize(qkv[..., OFF_Q:OFF_K].reshape(bsz, t, GDN_HEADS, GDN_DK)) * (GDN_DK ** -0.5)
    k = l2_normalize(qkv[..., OFF_K:OFF_V].reshape(bsz, t, GDN_HEADS, GDN_DK))
    v = qkv[..., OFF_V:OFF_Z].reshape(bsz, t, GDN_HEADS, GDN_DV)
    a = p[..., OFF_A:OFF_B].astype(jnp.float32).reshape(bsz, t, 2, GDN_HEADS)
    b = p[..., OFF_B:OFF_CONF].astype(jnp.float32).reshape(bsz, t, 2, GDN_HEADS)
    g = -jnp.exp(a_log.astype(jnp.float32)) * jax.nn.softplus(a + dt_bias.astype(jnp.float32))
    beta = jax.nn.sigmoid(b)
    return q, k, v, g, beta


def gdn_bidirectional(inputs, s0_f, s0_b):
    q, k, v, g, beta = inputs
    o_f, s_f = gdn_chunk_scan(q, k, v, g[:, :, 0], beta[:, :, 0], s0_f)
    flip = lambda a: jnp.flip(a, axis=1)
    o_b, s_b = gdn_chunk_scan(flip(q), flip(k), flip(v), flip(g[:, :, 1]), flip(beta[:, :, 1]), s0_b)
    return o_f + flip(o_b), s_f, s_b


def gdn_branch(o, p, norm_w):
    bsz, t = p.shape[:2]
    z = p[..., OFF_Z:OFF_A].astype(jnp.float32).reshape(bsz, t, GDN_HEADS, GDN_DV)
    y = rms_norm(o, norm_w.astype(jnp.float32)) * jax.nn.silu(z)
    return y.reshape(bsz, t, GDN_HEADS * GDN_DV).astype(p.dtype)


def conformer_branch(p, dw_w, dw_b, ln_g, ln_b):
    a, gt = jnp.split(p[..., OFF_CONF:OFF_FFT], 2, axis=-1)
    h = depthwise_conv(a * jax.nn.sigmoid(gt), dw_w) + dw_b
    return jax.nn.silu(group_ln(h, ln_g, ln_b, CONF_GROUPS))


def fourier_branch(p):
    bsz, t = p.shape[:2]
    h = p[..., OFF_FFT:OFF_SGU].astype(jnp.float32).reshape(bsz, t, FFT_GROUPS, BRANCH_W // FFT_GROUPS)
    y = jnp.fft.fft2(h, axes=(1, 3), norm='ortho').real
    return y.reshape(bsz, t, BRANCH_W).astype(p.dtype)


def sgu_branch(p, ln_g, ln_b, ws, bs):
    bsz, t = p.shape[:2]
    u, v = jnp.split(jax.nn.gelu(p[..., OFF_SGU:OFF_GATE]), 2, axis=-1)
    v = group_ln(v, ln_g, ln_b, SGU_GROUPS)
    vc = v.reshape(bsz, t // SGU_CHUNK, SGU_CHUNK, SGU_GROUPS, BRANCH_W // SGU_GROUPS)
    s = jnp.einsum('gpq,bnqgc->bnpgc', ws, vc) + bs.T[None, None, :, :, None]
    return u * s.reshape(bsz, t, BRANCH_W)


def token_mix_out(p, o_gdn, gdn_norm_w, conf_dw_w, conf_dw_b, conf_ln_g, conf_ln_b,
                  sgu_ln_g, sgu_ln_b, sgu_ws, sgu_bs, w_branch, w_out, b_out):
    bsz, t = p.shape[:2]
    branches = jnp.stack([gdn_branch(o_gdn, p, gdn_norm_w),
                          conformer_branch(p, conf_dw_w, conf_dw_b, conf_ln_g, conf_ln_b),
                          fourier_branch(p),
                          sgu_branch(p, sgu_ln_g, sgu_ln_b, sgu_ws, sgu_bs)], axis=2)
    proj = jnp.einsum('btiw,iwd->btid', branches, w_branch)
    gates = jax.nn.sigmoid(p[..., OFF_GATE:].reshape(bsz, t, N_BRANCH, D_MODEL))
    merged = jnp.sum(gates * proj, axis=2)
    return merged @ w_out + b_out


def hier_moe(h, rg_w, rg_b, re_w, re_b, w_gate, w_up, w_down):
    shp = h.shape
    tok = h.reshape(-1, shp[-1])
    glog = (tok @ rg_w + rg_b).astype(jnp.float32)
    gprob = jax.nn.softmax(glog, axis=-1)
    gsel = jnp.argmax(glog, axis=-1)
    gp = jnp.max(gprob, axis=-1, keepdims=True)
    elog = (tok @ re_w + re_b).astype(jnp.float32).reshape(-1, MOE_GROUPS, MOE_PER_GROUP)
    elog_sel = jnp.sum(elog * jax.nn.one_hot(gsel, MOE_GROUPS, dtype=jnp.float32)[:, :, None], axis=1)
    top_v, top_i = lax.top_k(elog_sel, MOE_TOPK)
    top_w = jax.nn.softmax(top_v, axis=-1) * gp
    eidx = gsel[:, None] * MOE_PER_GROUP + top_i
    comb = jnp.sum(jax.nn.one_hot(eidx, N_EXPERTS, dtype=jnp.float32) * top_w[..., None], axis=1)
    out = jnp.zeros(tok.shape, jnp.float32)
    for e in range(N_EXPERTS):
        hid = jax.nn.silu(tok @ w_gate[e]) * (tok @ w_up[e])
        out = out + comb[:, e:e + 1] * (hid @ w_down[e])
    return out.reshape(shp).astype(h.dtype)


def setup_inputs(seed: int = 0) -> dict:
    key = jax.random.key(seed)
    keys = jax.random.split(key, 48)
    counter = [0]

    def nk():
        k = keys[counter[0]]
        counter[0] += 1
        return k

    def nrm(shape, scale):
        return jax.random.normal(nk(), shape, jnp.float32) * scale

    L, D = DEPTH, D_MODEL
    x = nrm((BATCH, SEQ, D), 1.0)
    c = nrm((BATCH, D), 1.0)
    ctx = nrm((BATCH, CTX_LEN, D), 1.0)
    c_ctx = nrm((D,), 1.0)
    w_mod = nrm((L, D, 6 * D), 0.5 * D ** -0.5)
    b_mod = nrm((L, 6 * D), 0.02)
    w_in = nrm((L, D, N_IN), D ** -0.5)
    b_in = nrm((L, N_IN), 0.02)
    gdn_conv_w = nrm((L, SHORT_CONV, OFF_Z), SHORT_CONV ** -0.5)
    gdn_a_log = jnp.log(jax.random.uniform(nk(), (L, 2, GDN_HEADS), jnp.float32, 1.0, 16.0))
    dt = jnp.exp(jax.random.uniform(nk(), (L, 2, GDN_HEADS), jnp.float32, math.log(1e-3), math.log(1e-1)))
    gdn_dt_bias = dt + jnp.log(-jnp.expm1(-dt))
    gdn_norm_w = 1.0 + nrm((L, GDN_DV), 0.02)
    conf_dw_w = nrm((L, CONF_KW, BRANCH_W), CONF_KW ** -0.5)
    conf_dw_b = nrm((L, BRANCH_W), 0.02)
    conf_ln_g = 1.0 + nrm((L, BRANCH_W), 0.02)
    conf_ln_b = nrm((L, BRANCH_W), 0.02)
    sgu_ln_g = 1.0 + nrm((L, BRANCH_W), 0.02)
    sgu_ln_b = nrm((L, BRANCH_W), 0.02)
    sgu_ws = nrm((L, SGU_GROUPS, SGU_CHUNK, SGU_CHUNK), 0.5 * SGU_CHUNK ** -0.5)
    sgu_bs = 1.0 + nrm((L, SGU_GROUPS, SGU_CHUNK), 0.02)
    w_branch = nrm((L, N_BRANCH, BRANCH_W, D), DN_BETA * BRANCH_W ** -0.5)
    w_out = nrm((L, D, D), DN_BETA * D ** -0.5)
    b_out = nrm((L, D), 0.02)
    ln1_g = 1.0 + nrm((L, D), 0.02)
    ln1_b = nrm((L, D), 0.02)
    ln2_g = 1.0 + nrm((L, D), 0.02)
    ln2_b = nrm((L, D), 0.02)
    router_group_w = nrm((L, D, MOE_GROUPS), D ** -0.5)
    router_group_b = nrm((L, MOE_GROUPS), 0.01)
    router_expert_w = nrm((L, D, N_EXPERTS), D ** -0.5)
    router_expert_b = nrm((L, N_EXPERTS), 0.01)
    expert_w_gate = nrm((L, N_EXPERTS, D, EXPERT_HIDDEN), D ** -0.5)
    expert_w_up = nrm((L, N_EXPERTS, D, EXPERT_HIDDEN), D ** -0.5)
    expert_w_down = nrm((L, N_EXPERTS, EXPERT_HIDDEN, D), DN_BETA * EXPERT_HIDDEN ** -0.5)
    return {'x': x, 'c': c, 'ctx': ctx, 'c_ctx': c_ctx, 'w_mod': w_mod, 'b_mod': b_mod,
            'w_in': w_in, 'b_in': b_in, 'gdn_conv_w': gdn_conv_w, 'gdn_a_log': gdn_a_log,
            'gdn_dt_bias': gdn_dt_bias, 'gdn_norm_w': gdn_norm_w, 'conf_dw_w': conf_dw_w,
            'conf_dw_b': conf_dw_b, 'conf_ln_g': conf_ln_g, 'conf_ln_b': conf_ln_b,
            'sgu_ln_g': sgu_ln_g, 'sgu_ln_b': sgu_ln_b, 'sgu_ws': sgu_ws, 'sgu_bs': sgu_bs,
            'w_branch': w_branch, 'w_out': w_out, 'b_out': b_out, 'ln1_g': ln1_g, 'ln1_b': ln1_b,
            'ln2_g': ln2_g, 'ln2_b': ln2_b, 'router_group_w': router_group_w,
            'router_group_b': router_group_b, 'router_expert_w': router_expert_w,
            'router_expert_b': router_expert_b, 'expert_w_gate': expert_w_gate,
            'expert_w_up': expert_w_up, 'expert_w_down': expert_w_down}


def reference(x, c, ctx, c_ctx, w_mod, b_mod, w_in, b_in, gdn_conv_w, gdn_a_log, gdn_dt_bias, gdn_norm_w,
              conf_dw_w, conf_dw_b, conf_ln_g, conf_ln_b, sgu_ln_g, sgu_ln_b, sgu_ws, sgu_bs,
              w_branch, w_out, b_out, ln1_g, ln1_b, ln2_g, ln2_b, router_group_w, router_group_b,
              router_expert_w, router_expert_b, expert_w_gate, expert_w_up, expert_w_down):
    bsz, n_lat, d = x.shape
    h_lat = x + sincos_2d(n_lat, d).astype(x.dtype)[None]
    h_ctx = ctx
    s_zero = jnp.zeros((bsz, GDN_HEADS, GDN_DK, GDN_DV), jnp.float32)
    for l in range(DEPTH):
        last = l == DEPTH - 1
        mod_lat = jnp.split((jax.nn.silu(c) @ w_mod[l] + b_mod[l])[:, None, :], 6, axis=-1)
        mod_ctx = jnp.split((jax.nn.silu(c_ctx) @ w_mod[l] + b_mod[l])[None, None, :], 6, axis=-1)
        mix_w = (gdn_norm_w[l], conf_dw_w[l], conf_dw_b[l], conf_ln_g[l], conf_ln_b[l],
                 sgu_ln_g[l], sgu_ln_b[l], sgu_ws[l], sgu_bs[l], w_branch[l], w_out[l], b_out[l])
        moe_w = (router_group_w[l], router_group_b[l], router_expert_w[l], router_expert_b[l],
                 expert_w_gate[l], expert_w_up[l], expert_w_down[l])
        p_lat = modulate(h_lat, mod_lat[0], mod_lat[1]) @ w_in[l] + b_in[l]
        p_ctx = modulate(h_ctx, mod_ctx[0], mod_ctx[1]) @ w_in[l] + b_in[l]
        gdn_c = gdn_inputs(p_ctx, gdn_conv_w[l], gdn_a_log[l], gdn_dt_bias[l])
        o_ctx, s_f, s_b = gdn_bidirectional(gdn_c, s_zero, s_zero)
        gdn_l = gdn_inputs(p_lat, gdn_conv_w[l], gdn_a_log[l], gdn_dt_bias[l])
        o_lat, _, _ = gdn_bidirectional(gdn_l, s_f, s_b)
        y_lat = token_mix_out(p_lat, o_lat, *mix_w)
        h_lat = affine_ln(DN_ALPHA * h_lat + mod_lat[2] * y_lat, ln1_g[l], ln1_b[l])
        m_lat = hier_moe(modulate(h_lat, mod_lat[3], mod_lat[4]), *moe_w)
        h_lat = affine_ln(DN_ALPHA * h_lat + mod_lat[5] * m_lat, ln2_g[l], ln2_b[l])
        if not last:
            y_ctx = token_mix_out(p_ctx, o_ctx, *mix_w)
            h_ctx = affine_ln(DN_ALPHA * h_ctx + mod_ctx[2] * y_ctx, ln1_g[l], ln1_b[l])
            m_ctx = hier_moe(modulate(h_ctx, mod_ctx[3], mod_ctx[4]), *moe_w)
            h_ctx = affine_ln(DN_ALPHA * h_ctx + mod_ctx[5] * m_ctx, ln2_g[l], ln2_b[l])
    return h_lat
```

```python
import functools
import math

import numpy as np
import jax
import jax.numpy as jnp
from jax import lax
from jax.experimental import pallas as pl
from jax.experimental.pallas import tpu as pltpu

F32 = jnp.float32
BF16 = jnp.bfloat16
HIGHEST = lax.Precision.HIGHEST

D_MODEL = 1024
DEPTH = 2
GRID_W = 64
BRANCH_W = 512
N_BRANCH = 4
GDN_HEADS = 4
GDN_DK = 128
GDN_CHUNK = 64
SHORT_CONV = 5
CONF_KW = 31
SGU_CHUNK = 128
MOE_GROUPS = 4
MOE_PER_GROUP = 4
N_EXPERTS = 16
EXPERT_HIDDEN = 512
DN_ALPHA = (2 * DEPTH) ** 0.25
LN_EPS = 1e-6

OFF_Q, OFF_K, OFF_V, OFF_Z, OFF_A, OFF_B = 0, 512, 1024, 1536, 2048, 2056
OFF_CONF, OFF_FFT, OFF_SGU, OFF_GATE = 2064, 3088, 3600, 4624
N_IN = OFF_GATE + N_BRANCH * D_MODEL

LANE = 128
PB_GATE = 0
PB_Q, PB_K, PB_V, PB_Z = 32, 36, 40, 44
PB_CA, PB_CG = 48, 52
PB_FFT = 56
PB_SU, PB_SV = 60, 64
P_COLS = 68 * LANE

VMEM_LIMIT_BYTES = 52 * 1024 * 1024


def _cparams(sem):
    return pltpu.CompilerParams(dimension_semantics=sem, vmem_limit_bytes=VMEM_LIMIT_BYTES)


def _sigmoid(x):
    return 1.0 / (1.0 + jnp.exp(-x))


def _silu(x):
    return x * _sigmoid(x)


def _softplus(x):
    return jnp.maximum(x, 0.0) + jnp.log1p(jnp.exp(-jnp.abs(x)))


def _ln(x):
    mu = jnp.mean(x, axis=-1, keepdims=True)
    xc = x - mu
    var = jnp.mean(xc * xc, axis=-1, keepdims=True)
    return xc * lax.rsqrt(var + LN_EPS)


def _iota(shape, dim):
    return lax.broadcasted_iota(jnp.int32, shape, dim)


def _mod_kernel(c_ref, w_ref, b_ref, o_ref):
    s = _silu(c_ref[...])
    o_ref[...] = jnp.dot(s, w_ref[...], precision=HIGHEST, preferred_element_type=F32) + b_ref[...]


def _mod_call(cvec, w_mod, b_mod):
    nl, d, n6 = w_mod.shape
    rows = cvec.shape[0]
    tn = 512
    return pl.pallas_call(
        _mod_kernel,
        grid=(nl, n6 // tn),
        in_specs=[pl.BlockSpec((rows, d), lambda l, j: (0, 0)),
                  pl.BlockSpec((None, d, tn), lambda l, j: (l, 0, j)),
                  pl.BlockSpec((None, 1, tn), lambda l, j: (l, 0, j))],
        out_specs=pl.BlockSpec((None, rows, tn), lambda l, j: (l, 0, j)),
        out_shape=jax.ShapeDtypeStruct((nl, rows, n6), F32),
        compiler_params=_cparams(("arbitrary", "arbitrary")),
        name="mod",
    )(cvec, w_mod, b_mod.reshape(nl, 1, n6))


def _inproj_kernel(h_ref, sh_ref, sc_ref, w_ref, b_ref, wab_ref, bab_ref, p_ref, ab_ref, xs_ref):
    @pl.when(pl.program_id(1) == 0)
    def _():
        xm = _ln(h_ref[...]) * (1.0 + sc_ref[...]) + sh_ref[...]
        xs_ref[...] = xm.astype(BF16)
        ab_ref[...] = jnp.dot(xm, wab_ref[...], precision=HIGHEST, preferred_element_type=F32) + bab_ref[...]

    p_ref[...] = jnp.dot(xs_ref[...], w_ref[...], preferred_element_type=F32) + b_ref[...]


def _inproj_call(h, row_off, n_rows, shift, scale, tiles_per_group, w, b, wab, bab, tm, tn):
    d = h.shape[1]
    ncols = w.shape[1]
    ngrp = shift.shape[0]
    gidx = lambda i, j: (jnp.minimum(i // tiles_per_group, ngrp - 1), 0, 0)
    return pl.pallas_call(
        _inproj_kernel,
        grid=(n_rows // tm, ncols // tn),
        in_specs=[pl.BlockSpec((tm, d), lambda i, j: (i + row_off // tm, 0)),
                  pl.BlockSpec((None, 1, d), gidx),
                  pl.BlockSpec((None, 1, d), gidx),
                  pl.BlockSpec((d, tn), lambda i, j: (0, j)),
                  pl.BlockSpec((1, tn), lambda i, j: (0, j)),
                  pl.BlockSpec((d, LANE), lambda i, j: (0, 0)),
                  pl.BlockSpec((1, LANE), lambda i, j: (0, 0))],
        out_specs=[pl.BlockSpec((tm, tn), lambda i, j: (i, j)),
                   pl.BlockSpec((tm, LANE), lambda i, j: (i, 0))],
        out_shape=[jax.ShapeDtypeStruct((n_rows, ncols), F32),
                   jax.ShapeDtypeStruct((n_rows, LANE), F32)],
        scratch_shapes=[pltpu.VMEM((tm, d), BF16)],
        compiler_params=_cparams(("arbitrary", "arbitrary")),
        name="inproj",
    )(h, shift, scale, w, b, wab, bab)


def _gdn_kernel(qp_ref, kp_ref, vp_ref, zp_ref, cwq_ref, cwk_ref, cwv_ref, ab_ref, prm_ref, nw_ref, s0_ref,
                y_ref, sout_ref,
                pad_ref, qs_ref, ks_ref, vs_ref, gb_ref, mb_ref, qa_ref, egl_ref, o_ref, *, seq, with_y):
    cs = GDN_CHUNK
    nc = seq // cs
    dk = GDN_DK
    head = pl.program_id(1)

    pad_ref[pl.ds(0, 8), :] = jnp.zeros((8, dk), F32)
    pad_ref[pl.ds(8 + seq, 8), :] = jnp.zeros((8, dk), F32)

    def conv_silu(x_ref, cw_ref):
        pad_ref[pl.ds(8, seq), :] = x_ref[...]
        half = SHORT_CONV // 2
        acc = None
        for j in range(SHORT_CONV):
            term = cw_ref[pl.ds(j, 1), :] * pad_ref[pl.ds(8 + j - half, seq), :]
            acc = term if acc is None else acc + term
        return _silu(acc)

    def l2n(x):
        return x * lax.rsqrt(jnp.sum(x * x, axis=-1, keepdims=True) + LN_EPS)

    qs_ref[...] = l2n(conv_silu(qp_ref, cwq_ref)) * (dk ** -0.5)
    ks_ref[...] = l2n(conv_silu(kp_ref, cwk_ref))
    vs_ref[...] = conv_silu(vp_ref, cwv_ref)

    lane_t = _iota((seq, LANE), 1)
    xab = ab_ref[...]
    gval = -prm_ref[pl.ds(0, 1), :] * _softplus(xab + prm_ref[pl.ds(1, 1), :])
    gb_ref[...] = jnp.where(lane_t < 2 * GDN_HEADS, gval, _sigmoid(xab))

    row = _iota((cs, cs), 0)
    col = _iota((cs, cs), 1)
    eye = row == col
    lane_c = _iota((cs, LANE), 1)
    row_c = _iota((cs, 1), 0)

    def precompute(direction):
        if direction == 0:
            incl, strict, last = row >= col, row > col, cs - 1
        else:
            incl, strict, last = row <= col, row < col, 0
        tri = jnp.where(incl, 1.0, 0.0).astype(F32)
        cg = direction * GDN_HEADS + head
        cb = 2 * GDN_HEADS + direction * GDN_HEADS + head

        def body(n, carry):
            r0 = pl.multiple_of(n * cs, cs)
            graw = gb_ref[pl.ds(r0, cs), :]
            gcum = jnp.dot(tri, graw, precision=HIGHEST, preferred_element_type=F32)
            gcol = jnp.sum(jnp.where(lane_c == cg, gcum, 0.0), axis=1, keepdims=True)
            bcol = jnp.sum(jnp.where(lane_c == cb, graw, 0.0), axis=1, keepdims=True)
            grow = jnp.sum(jnp.where(eye, gcol, 0.0), axis=0, keepdims=True)
            decay = jnp.where(incl, jnp.exp(jnp.minimum(gcol - grow, 0.0)), 0.0)
            glast = jnp.sum(jnp.where(row_c == last, gcol, 0.0), axis=0, keepdims=True)
            egc = jnp.exp(gcol)
            ekd = jnp.exp(glast - gcol)

            kc = ks_ref[pl.ds(r0, cs), :]
            qc = qs_ref[pl.ds(r0, cs), :]
            vc = vs_ref[pl.ds(r0, cs), :]
            kb = kc * bcol
            qk = lax.dot_general(jnp.concatenate([kb, qc], axis=0).astype(BF16), kc.astype(BF16),
                                 (((1,), (1,)), ((), ())), preferred_element_type=F32)
            lmat = jnp.where(strict, qk[:cs] * decay, 0.0)
            attn = qk[cs:] * decay
            x = jnp.concatenate([vc * bcol, kb * egc], axis=1)

            pows = [lmat]
            for _ in range(int(math.log2(cs)) - 1):
                pows.append(jnp.dot(pows[-1], pows[-1], precision=HIGHEST, preferred_element_type=F32))
            for pw in reversed(pows[1:]):
                x = x + jnp.dot(pw, x, precision=HIGHEST, preferred_element_type=F32)
            x = x - jnp.dot(lmat, x, precision=HIGHEST, preferred_element_type=F32)

            xb = x.astype(BF16)
            kd = (kc * ekd).astype(BF16)
            mb_ref[direction, n] = lax.dot_general(kd, xb, (((0,), (0,)), ((), ())),
                                                   preferred_element_type=F32)
            au_aw = jnp.dot(attn.astype(BF16), xb, preferred_element_type=F32)
            qa_ref[direction, pl.ds(r0, cs), :] = jnp.concatenate(
                [au_aw[:, :dk], qc * egc - au_aw[:, dk:]], axis=1)
            egl_ref[direction, n] = jnp.broadcast_to(jnp.exp(glast), (8, LANE))
            return carry

        lax.fori_loop(0, nc, body, 0)

    precompute(0)
    precompute(1)

    def scan_body(i, carry):
        new = []
        for direction in range(2):
            s = carry[direction]
            n = i if direction == 0 else nc - 1 - i
            r0 = pl.multiple_of(n * cs, cs)
            sb = s.astype(BF16)
            qa = qa_ref[direction, pl.ds(r0, cs), :]
            mb = mb_ref[direction, n]
            if with_y:
                o = jnp.dot(qa[:, dk:].astype(BF16), sb, preferred_element_type=F32) + qa[:, :dk]
                if direction == 0:
                    o_ref[0, pl.ds(r0, cs), :] = o
                else:
                    o_ref[1, pl.ds(r0, cs), :] = o
            egl = egl_ref[direction, n][0:1, :]
            s = egl * s - jnp.dot(mb[:, dk:].astype(BF16), sb, preferred_element_type=F32) + mb[:, :dk]
            new.append(s)
        return tuple(new)

    s_fin = lax.fori_loop(0, nc, scan_body, (s0_ref[0], s0_ref[1]))
    sout_ref[0] = s_fin[0]
    sout_ref[1] = s_fin[1]

    if with_y:
        o = o_ref[0] + o_ref[1]
        rms = o * lax.rsqrt(jnp.mean(o * o, axis=-1, keepdims=True) + LN_EPS) * nw_ref[...]
        y_ref[...] = (rms * _silu(zp_ref[...])).astype(y_ref.dtype)
    else:
        y_ref[...] = jnp.zeros(y_ref.shape, y_ref.dtype)


def _gdn_call(p3, ab3, boff, bsz, blk_q, blk_k, blk_v, blk_z, conv_w8, prm, norm_w, s0, with_y):
    seq = p3.shape[1]
    dk = GDN_DK
    nc = seq // GDN_CHUNK
    pblk = lambda base: pl.BlockSpec((None, seq, dk), lambda b, h: (b + boff, 0, base + h))
    cblk = lambda base: pl.BlockSpec((8, dk), lambda b, h: (0, base + h))
    ydim = seq if with_y else 8
    kern = functools.partial(_gdn_kernel, seq=seq, with_y=with_y)
    return pl.pallas_call(
        kern,
        grid=(bsz, GDN_HEADS),
        in_specs=[pblk(blk_q), pblk(blk_k), pblk(blk_v), pblk(blk_z),
                  cblk(0), cblk(GDN_HEADS), cblk(2 * GDN_HEADS),
                  pl.BlockSpec((None, seq, LANE), lambda b, h: (b + boff, 0, 0)),
                  pl.BlockSpec((8, LANE), lambda b, h: (0, 0)),
                  pl.BlockSpec((1, dk), lambda b, h: (0, 0)),
                  pl.BlockSpec((None, None, 2, dk, dk), lambda b, h: (b, h, 0, 0, 0))],
        out_specs=[pl.BlockSpec((None, ydim, dk), lambda b, h: (b, 0, h)),
                   pl.BlockSpec((None, None, 2, dk, dk), lambda b, h: (b, h, 0, 0, 0))],
        out_shape=[jax.ShapeDtypeStruct((bsz, ydim, GDN_HEADS * dk), BF16),
                   jax.ShapeDtypeStruct((bsz, GDN_HEADS, 2, dk, dk), F32)],
        scratch_shapes=[pltpu.VMEM((seq + 16, dk), F32),
                        pltpu.VMEM((seq, dk), F32), pltpu.VMEM((seq, dk), F32), pltpu.VMEM((seq, dk), F32),
                        pltpu.VMEM((seq, LANE), F32),
                        pltpu.VMEM((2, nc, dk, 2 * dk), F32),
                        pltpu.VMEM((2, seq, 2 * dk), F32),
                        pltpu.VMEM((2, nc, 8, LANE), F32),
                        pltpu.VMEM((2, seq, dk), F32)],
        compiler_params=_cparams(("arbitrary", "arbitrary")),
        name="gdn",
    )(p3, p3, p3, p3, conv_w8, conv_w8, conv_w8, ab3, prm, norm_w, s0)


def _conf_kernel(a_ref, g_ref, w_ref, b_ref, lg_ref, lb_ref, y_ref, pad_ref, *, seq):
    halo = 16
    pad_ref[pl.ds(0, halo), :] = jnp.zeros((halo, LANE), F32)
    pad_ref[pl.ds(halo + seq, halo), :] = jnp.zeros((halo, LANE), F32)
    pad_ref[pl.ds(halo, seq), :] = a_ref[...] * _sigmoid(g_ref[...])
    half = CONF_KW // 2
    acc = None
    for j in range(CONF_KW):
        term = w_ref[pl.ds(j, 1), :] * pad_ref[pl.ds(halo + j - half, seq), :]
        acc = term if acc is None else acc + term
    hcv = acc + b_ref[...]
    y_ref[...] = _silu(_ln(hcv) * lg_ref[...] + lb_ref[...]).astype(y_ref.dtype)


def _conf_call(p3, boff, bsz, w32, bias, ln_g, ln_b):
    seq = p3.shape[1]
    ng = BRANCH_W // LANE
    vec = pl.BlockSpec((1, LANE), lambda b, g: (0, g))
    return pl.pallas_call(
        functools.partial(_conf_kernel, seq=seq),
        grid=(bsz, ng),
        in_specs=[pl.BlockSpec((None, seq, LANE), lambda b, g: (b + boff, 0, PB_CA + g)),
                  pl.BlockSpec((None, seq, LANE), lambda b, g: (b + boff, 0, PB_CG + g)),
                  pl.BlockSpec((32, LANE), lambda b, g: (0, g)), vec, vec, vec],
        out_specs=pl.BlockSpec((None, seq, LANE), lambda b, g: (b, 0, g)),
        out_shape=jax.ShapeDtypeStruct((bsz, seq, BRANCH_W), BF16),
        scratch_shapes=[pltpu.VMEM((seq + 32, LANE), F32)],
        compiler_params=_cparams(("arbitrary", "arbitrary")),
        name="conf",
    )(p3, p3, w32, bias, ln_g, ln_b)


def _fft_chan_kernel(x_ref, cs_ref, o_ref):
    y = jnp.dot(x_ref[...].astype(BF16), cs_ref[...], preferred_element_type=F32)
    o_ref[0] = y[:, :LANE].astype(o_ref.dtype)
    o_ref[1] = y[:, LANE:].astype(o_ref.dtype)


def _fft_chan_call(p3, boff, bsz, cs_mat):
    seq = p3.shape[1]
    ng = BRANCH_W // LANE
    return pl.pallas_call(
        _fft_chan_kernel,
        grid=(bsz, ng),
        in_specs=[pl.BlockSpec((None, seq, LANE), lambda b, g: (b + boff, 0, PB_FFT + g)),
                  pl.BlockSpec((LANE, 2 * LANE), lambda b, g: (0, 0))],
        out_specs=pl.BlockSpec((2, seq, LANE), lambda b, g: (0, 0, b * ng + g)),
        out_shape=jax.ShapeDtypeStruct((2, seq, bsz * BRANCH_W), BF16),
        compiler_params=_cparams(("arbitrary", "arbitrary")),
        name="fft_chan",
    )(p3, cs_mat)


def _matmul_kernel(a_ref, b_ref, o_ref):
    o_ref[...] = jnp.dot(a_ref[...], b_ref[...], preferred_element_type=F32).astype(o_ref.dtype)


def _fft_pos_call(wpos, xcs2):
    m, k = wpos.shape
    n = xcs2.shape[1]
    tm = min(m, 512)
    tn = min(n, 1024)
    return pl.pallas_call(
        _matmul_kernel,
        grid=(m // tm, n // tn),
        in_specs=[pl.BlockSpec((tm, k), lambda i, j: (i, 0)),
                  pl.BlockSpec((k, tn), lambda i, j: (0, j))],
        out_specs=pl.BlockSpec((tm, tn), lambda i, j: (i, j)),
        out_shape=jax.ShapeDtypeStruct((m, n), BF16),
        compiler_params=_cparams(("arbitrary", "arbitrary")),
        name="fft_pos",
    )(wpos, xcs2)


def _gelu(x):
    return 0.5 * x * (1.0 + jnp.tanh(math.sqrt(2.0 / math.pi) * (x + 0.044715 * (x * x * x))))


def _sgu_kernel(u_ref, v_ref, lg_ref, lb_ref, ws_ref, bs_ref, y_ref, *, seq):
    ws = ws_ref[...].astype(BF16)
    bsb = bs_ref[...]
    for n in range(seq // SGU_CHUNK):
        rows = pl.ds(n * SGU_CHUNK, SGU_CHUNK)
        vn = _ln(_gelu(v_ref[rows, :])) * lg_ref[...] + lb_ref[...]
        s = jnp.dot(ws, vn.astype(BF16), preferred_element_type=F32) + bsb
        y_ref[rows, :] = (_gelu(u_ref[rows, :]) * s).astype(y_ref.dtype)


def _sgu_call(p3, boff, bsz, ln_g, ln_b, ws, bsb):
    seq = p3.shape[1]
    ng = BRANCH_W // LANE
    vec = pl.BlockSpec((1, LANE), lambda b, g: (0, g))
    return pl.pallas_call(
        functools.partial(_sgu_kernel, seq=seq),
        grid=(bsz, ng),
        in_specs=[pl.BlockSpec((None, seq, LANE), lambda b, g: (b + boff, 0, PB_SU + g)),
                  pl.BlockSpec((None, seq, LANE), lambda b, g: (b + boff, 0, PB_SV + g)),
                  vec, vec,
                  pl.BlockSpec((None, SGU_CHUNK, SGU_CHUNK), lambda b, g: (g, 0, 0)),
                  pl.BlockSpec((None, SGU_CHUNK, LANE), lambda b, g: (g, 0, 0))],
        out_specs=pl.BlockSpec((None, seq, LANE), lambda b, g: (b, 0, g)),
        out_shape=jax.ShapeDtypeStruct((bsz, seq, BRANCH_W), BF16),
        compiler_params=_cparams(("arbitrary", "arbitrary")),
        name="sgu",
    )(p3, p3, ln_g, ln_b, ws, bsb)


def _merge_kernel(yg_ref, yc_ref, yf_ref, ys_ref, gate_ref, h_ref, g1_ref, sh2_ref, sc2_ref,
                  wb_ref, wo_ref, bo_ref, lg_ref, lb_ref, wr_ref, br_ref,
                  h1_ref, xm_ref, comb_ref):
    d = D_MODEL
    acc = None
    for i, y_ref in enumerate((yg_ref, yc_ref, yf_ref, ys_ref)):
        proj = jnp.dot(y_ref[...], wb_ref[i], preferred_element_type=F32)
        term = _sigmoid(gate_ref[:, i * d:(i + 1) * d]) * proj
        acc = term if acc is None else acc + term
    y = jnp.dot(acc.astype(BF16), wo_ref[...], preferred_element_type=F32) + bo_ref[...]
    h1 = _ln(DN_ALPHA * h_ref[...] + g1_ref[...] * y) * lg_ref[...] + lb_ref[...]
    h1_ref[...] = h1
    xm = _ln(h1) * (1.0 + sc2_ref[...]) + sh2_ref[...]
    xm_ref[...] = xm.astype(BF16)

    logits = jnp.dot(xm, wr_ref[...], precision=HIGHEST, preferred_element_type=F32) + br_ref[...]
    lane = _iota(logits.shape, 1)
    ninf = -jnp.inf
    big = jnp.int32(1 << 20)
    gl = jnp.where(lane < MOE_GROUPS, logits, ninf)
    gmax = jnp.max(gl, axis=-1, keepdims=True)
    gsel = jnp.min(jnp.where(gl == gmax, lane, big), axis=-1, keepdims=True)
    gp = 1.0 / jnp.sum(jnp.exp(gl - gmax), axis=-1, keepdims=True)
    lo = MOE_GROUPS + gsel * MOE_PER_GROUP
    el = jnp.where((lane >= lo) & (lane < lo + MOE_PER_GROUP), logits, ninf)
    v1 = jnp.max(el, axis=-1, keepdims=True)
    i1 = jnp.min(jnp.where(el == v1, lane, big), axis=-1, keepdims=True)
    el2 = jnp.where(lane == i1, ninf, el)
    v2 = jnp.max(el2, axis=-1, keepdims=True)
    i2 = jnp.min(jnp.where(el2 == v2, lane, big), axis=-1, keepdims=True)
    e21 = jnp.exp(v2 - v1)
    w1 = gp / (1.0 + e21)
    w2 = gp * e21 / (1.0 + e21)
    comb_ref[...] = jnp.where(lane == i1, w1, 0.0) + jnp.where(lane == i2, w2, 0.0)


def _merge_call(row_off, n_rows, seq, yg, yc, yf, ys, p, h, g1, sh2, sc2, tiles_per_group,
                wb, wo, bo, lg, lb, wr, br, tm):
    d = D_MODEL
    ngrp = g1.shape[0]
    tps = seq // tm
    toff = row_off // tm
    gidx = lambda i: (jnp.minimum(i // tiles_per_group, ngrp - 1), 0, 0)
    rowblk = lambda w: pl.BlockSpec((tm, w), lambda i: (i, 0))
    const2 = lambda a: pl.BlockSpec(a.shape, lambda i: (0, 0))
    return pl.pallas_call(
        _merge_kernel,
        grid=(n_rows // tm,),
        in_specs=[rowblk(BRANCH_W), rowblk(BRANCH_W),
                  pl.BlockSpec((tm, BRANCH_W), lambda i: (i % tps, i // tps)),
                  rowblk(BRANCH_W),
                  pl.BlockSpec((tm, N_BRANCH * d), lambda i: (i + toff, PB_GATE)),
                  pl.BlockSpec((tm, d), lambda i: (i + toff, 0)),
                  pl.BlockSpec((None, 1, d), gidx), pl.BlockSpec((None, 1, d), gidx),
                  pl.BlockSpec((None, 1, d), gidx),
                  pl.BlockSpec(wb.shape, lambda i: (0, 0, 0)),
                  const2(wo), const2(bo), const2(lg), const2(lb), const2(wr), const2(br)],
        out_specs=[rowblk(d), rowblk(d), rowblk(LANE)],
        out_shape=[jax.ShapeDtypeStruct((n_rows, d), F32),
                   jax.ShapeDtypeStruct((n_rows, d), BF16),
                   jax.ShapeDtypeStruct((n_rows, LANE), F32)],
        compiler_params=_cparams(("arbitrary",)),
        name="merge",
    )(yg, yc, yf, ys, p, h, g1, sh2, sc2, wb, wo, bo, lg, lb, wr, br)


def _moe_kernel(x_ref, comb_ref, wg_ref, wu_ref, wd_ref, h1_ref, g2_ref, lg_ref, lb_ref, o_ref, acc_ref):
    e = pl.program_id(1)

    @pl.when(e == 0)
    def _():
        acc_ref[...] = jnp.zeros(acc_ref.shape, F32)

    x = x_ref[...]
    hg = jnp.dot(x, wg_ref[...].astype(BF16), preferred_element_type=F32)
    hu = jnp.dot(x, wu_ref[...].astype(BF16), preferred_element_type=F32)
    comb = comb_ref[...]
    lane = _iota(comb.shape, 1)
    ce = jnp.sum(jnp.where(lane == MOE_GROUPS + e, comb, 0.0), axis=-1, keepdims=True)
    hid = (_silu(hg) * hu * ce).astype(BF16)
    acc_ref[...] += jnp.dot(hid, wd_ref[...].astype(BF16), preferred_element_type=F32)

    @pl.when(e == N_EXPERTS - 1)
    def _():
        o_ref[...] = _ln(DN_ALPHA * h1_ref[...] + g2_ref[...] * acc_ref[...]) * lg_ref[...] + lb_ref[...]


def _moe_call(n_rows, xm, comb, wg, wu, wd, h1, g2, tiles_per_group, lg, lb, tm):
    d = D_MODEL
    ngrp = g2.shape[0]
    gidx = lambda i, e: (jnp.minimum(i // tiles_per_group, ngrp - 1), 0, 0)
    return pl.pallas_call(
        _moe_kernel,
        grid=(n_rows // tm, N_EXPERTS),
        in_specs=[pl.BlockSpec((tm, d), lambda i, e: (i, 0)),
                  pl.BlockSpec((tm, LANE), lambda i, e: (i, 0)),
                  pl.BlockSpec((None, d, EXPERT_HIDDEN), lambda i, e: (e, 0, 0)),
                  pl.BlockSpec((None, d, EXPERT_HIDDEN), lambda i, e: (e, 0, 0)),
                  pl.BlockSpec((None, EXPERT_HIDDEN, d), lambda i, e: (e, 0, 0)),
                  pl.BlockSpec((tm, d), lambda i, e: (i, 0)),
                  pl.BlockSpec((None, 1, d), gidx),
                  pl.BlockSpec((1, d), lambda i, e: (0, 0)),
                  pl.BlockSpec((1, d), lambda i, e: (0, 0))],
        out_specs=pl.BlockSpec((tm, d), lambda i, e: (i, 0)),
        out_shape=jax.ShapeDtypeStruct((n_rows, d), F32),
        scratch_shapes=[pltpu.VMEM((tm, d), F32)],
        compiler_params=_cparams(("arbitrary", "arbitrary")),
        name="moe",
    )(xm, comb, wg, wu, wd, h1, g2, lg, lb)


def _sincos_table(n, d):
    rows = n // GRID_W
    row = np.broadcast_to(np.arange(rows, dtype=np.float32)[:, None], (rows, GRID_W)).reshape(-1)
    col = np.broadcast_to(np.arange(GRID_W, dtype=np.float32)[None, :], (rows, GRID_W)).reshape(-1)
    quarter = d // 4
    omega = (1.0 / (10000.0 ** (np.arange(quarter, dtype=np.float32) / np.float32(quarter)))).astype(np.float32)

    def enc(pos):
        ang = (pos[:, None] * omega[None, :]).astype(np.float32)
        return np.concatenate([np.sin(ang), np.cos(ang)], -1)

    return np.concatenate([enc(row), enc(col)], -1).astype(np.float32)


def _dft_mats(n):
    k = np.arange(n, dtype=np.int64)
    ang = 2.0 * np.pi * ((k[:, None] * k[None, :]) % n).astype(np.float64) / n
    return np.cos(ang), np.sin(ang)


def _fft_consts(seq):
    cc, sc = _dft_mats(LANE)
    cs_mat = np.concatenate([cc, sc], axis=1)
    ct, st = _dft_mats(seq)
    wpos = np.concatenate([ct, -st], axis=1) / math.sqrt(seq * LANE)
    return jnp.asarray(cs_mat, dtype=BF16), jnp.asarray(wpos, dtype=BF16)


def _pick_tile(preferred, *dims):
    t = preferred
    while any(dim % t for dim in dims):
        t //= 2
    return t


def _to_kernel_cols(a):
    segs = [(OFF_GATE, N_IN), (OFF_Q, OFF_A), (OFF_CONF, OFF_GATE)]
    return jnp.concatenate([a[..., lo:hi] for lo, hi in segs], axis=-1)


def kernel(x, c, ctx, c_ctx, w_mod, b_mod, w_in, b_in, gdn_conv_w, gdn_a_log, gdn_dt_bias, gdn_norm_w,
           conf_dw_w, conf_dw_b, conf_ln_g, conf_ln_b, sgu_ln_g, sgu_ln_b, sgu_ws, sgu_bs,
           w_branch, w_out, b_out, ln1_g, ln1_b, ln2_g, ln2_b, router_group_w, router_group_b,
           router_expert_w, router_expert_b, expert_w_gate, expert_w_up, expert_w_down):
    bsz, seq, d = x.shape
    cseq = ctx.shape[1]
    n_lat = bsz * seq
    n_ctx = bsz * cseq
    nl = w_mod.shape[0]
    tm_in = _pick_tile(1024, seq, n_ctx)
    tm_merge = _pick_tile(256, seq, cseq)
    tm_moe = _pick_tile(1024, seq, n_ctx)

    h_all = jnp.concatenate([(x + jnp.asarray(_sincos_table(seq, d))[None]).reshape(n_lat, d),
                             ctx.reshape(n_ctx, d)], axis=0)

    cvec = jnp.concatenate([c, c_ctx[None, :], jnp.zeros((16 - bsz - 1, d), F32)], axis=0)
    mod = _mod_call(cvec, w_mod, b_mod)
    ngrp = bsz + 1

    def modvec(l, k):
        return mod[l, :ngrp, k * d:(k + 1) * d].reshape(ngrp, 1, d)

    zeros_state = jnp.zeros((bsz, GDN_HEADS, 2, GDN_DK, GDN_DK), F32)
    cs_mat_l, wpos_l = _fft_consts(seq)
    cs_mat_c, wpos_c = _fft_consts(cseq)

    for l in range(nl):
        last = l == nl - 1
        w_main = _to_kernel_cols(w_in[l]).astype(BF16)
        b_main = _to_kernel_cols(b_in[l]).reshape(1, -1)
        n_ab = OFF_CONF - OFF_A
        wab = jnp.pad(w_in[l][:, OFF_A:OFF_CONF], ((0, 0), (0, LANE - n_ab)))
        bab = jnp.pad(b_in[l][OFF_A:OFF_CONF], (0, LANE - n_ab)).reshape(1, LANE)
        shift1, scale1, gate1 = modvec(l, 0), modvec(l, 1), modvec(l, 2)
        shift2, scale2, gate2 = modvec(l, 3), modvec(l, 4), modvec(l, 5)

        conv_w8 = jnp.pad(gdn_conv_w[l], ((0, 8 - SHORT_CONV), (0, 0)))
        a_row = jnp.pad(jnp.exp(gdn_a_log[l].astype(F32)).reshape(-1), (0, LANE - 2 * GDN_HEADS))
        dt_row = jnp.pad(gdn_dt_bias[l].astype(F32).reshape(-1), (0, LANE - 2 * GDN_HEADS))
        prm = jnp.concatenate([a_row[None], dt_row[None], jnp.zeros((6, LANE), F32)], axis=0)
        norm_w = gdn_norm_w[l].reshape(1, GDN_DK)

        if not last:
            n_rows = n_lat + n_ctx
            p, ab = _inproj_call(h_all, 0, n_rows, shift1, scale1, seq // tm_in, w_main, b_main, wab, bab,
                                 tm_in, 512)
            p_ctx3 = p.reshape(n_rows // cseq, cseq, P_COLS)
            ab_ctx3 = ab.reshape(n_rows // cseq, cseq, LANE)
            ctx_off = n_lat // cseq
            blk = (PB_Q, PB_K, PB_V, PB_Z)
        else:
            n_rows = n_lat
            p, ab = _inproj_call(h_all, 0, n_rows, shift1, scale1, seq // tm_in, w_main, b_main, wab, bab,
                                 tm_in, 512)
            w_qkv = w_main[:, PB_Q * LANE:PB_Z * LANE]
            b_qkv = b_main[:, PB_Q * LANE:PB_Z * LANE]
            pc, abc = _inproj_call(h_all, n_lat, n_ctx, shift1[bsz:], scale1[bsz:], 1, w_qkv, b_qkv,
                                   wab, bab, tm_in, 512)
            p_ctx3 = pc.reshape(bsz, cseq, 3 * BRANCH_W)
            ab_ctx3 = abc.reshape(bsz, cseq, LANE)
            ctx_off = 0
            blk = (0, GDN_HEADS, 2 * GDN_HEADS, 0)
        p_lat3 = p.reshape(n_rows // seq, seq, P_COLS)
        ab_lat3 = ab.reshape(n_rows // seq, seq, LANE)

        yg_ctx, s_ctx = _gdn_call(p_ctx3, ab_ctx3, ctx_off, bsz, *blk, conv_w8, prm, norm_w, zeros_state,
                                  not last)
        yg_lat, _ = _gdn_call(p_lat3, ab_lat3, 0, bsz, PB_Q, PB_K, PB_V, PB_Z, conv_w8, prm, norm_w,
                              s_ctx, True)

        conf_w32 = jnp.pad(conf_dw_w[l], ((0, 32 - CONF_KW), (0, 0)))
        conf_args = (conf_w32, conf_dw_b[l].reshape(1, -1), conf_ln_g[l].reshape(1, -1),
                     conf_ln_b[l].reshape(1, -1))
        sgu_args = (sgu_ln_g[l].reshape(1, -1), sgu_ln_b[l].reshape(1, -1), sgu_ws[l],
                    jnp.broadcast_to(sgu_bs[l][:, :, None], (BRANCH_W // LANE, SGU_CHUNK, LANE)))

        def seq_branches(p3, boff, cs_mat, wpos):
            t_ = p3.shape[1]
            yc = _conf_call(p3, boff, bsz, *conf_args).reshape(bsz * t_, BRANCH_W)
            xcs = _fft_chan_call(p3, boff, bsz, cs_mat).reshape(2 * t_, bsz * BRANCH_W)
            yf = _fft_pos_call(wpos, xcs)
            ys = _sgu_call(p3, boff, bsz, *sgu_args).reshape(bsz * t_, BRANCH_W)
            return yc, yf, ys

        wb = w_branch[l].astype(BF16)
        wo = w_out[l].astype(BF16)
        wr = jnp.pad(jnp.concatenate([router_group_w[l], router_expert_w[l]], axis=1),
                     ((0, 0), (0, LANE - MOE_GROUPS - N_EXPERTS)))
        br = jnp.pad(jnp.concatenate([router_group_b[l], router_expert_b[l]]),
                     (0, LANE - MOE_GROUPS - N_EXPERTS)).reshape(1, LANE)
        merge_w = (wb, wo, b_out[l].reshape(1, d), ln1_g[l].reshape(1, d), ln1_b[l].reshape(1, d), wr, br)
        moe_w = (expert_w_gate[l], expert_w_up[l], expert_w_down[l])
        ln2 = (ln2_g[l].reshape(1, d), ln2_b[l].reshape(1, d))

        def channel_mix(row_off, rows, seq_len, yg, yc, yf, ys, g1, sh2, sc2, g2, tpg_merge, tpg_moe):
            h1, xm, comb = _merge_call(row_off, rows, seq_len, yg, yc, yf, ys, p, h_all, g1, sh2, sc2,
                                       tpg_merge, *merge_w, tm_merge)
            return _moe_call(rows, xm, comb, *moe_w, h1, g2, tpg_moe, *ln2, tm_moe)

        yc, yf, ys = seq_branches(p_lat3, 0, cs_mat_l, wpos_l)
        h_lat = channel_mix(0, n_lat, seq, yg_lat.reshape(n_lat, BRANCH_W), yc, yf, ys,
                            gate1, shift2, scale2, gate2, seq // tm_merge, seq // tm_moe)
        if last:
            return h_lat.reshape(bsz, seq, d)

        ycc, yfc, ysc = seq_branches(p_ctx3, ctx_off, cs_mat_c, wpos_c)
        h_ctx = channel_mix(n_lat, n_ctx, cseq, yg_ctx.reshape(n_ctx, BRANCH_W), ycc, yfc, ysc,
                            gate1[bsz:], shift2[bsz:], scale2[bsz:], gate2[bsz:], 1, 1)
        h_all = jnp.concatenate([h_lat, h_ctx], axis=0)
```

```python
import functools
import math

import numpy as np
import jax
import jax.numpy as jnp
from jax import lax
from jax.experimental import pallas as pl
from jax.experimental.pallas import tpu as pltpu

F32 = jnp.float32
BF16 = jnp.bfloat16
HIGHEST = lax.Precision.HIGHEST

D_MODEL = 1024
DEPTH = 2
GRID_W = 64
BRANCH_W = 512
N_BRANCH = 4
GDN_HEADS = 4
GDN_DK = 128
GDN_CHUNK = 64
SHORT_CONV = 5
CONF_KW = 31
SGU_CHUNK = 128
MOE_GROUPS = 4
MOE_PER_GROUP = 4
N_EXPERTS = 16
EXPERT_HIDDEN = 512
DN_ALPHA = (2 * DEPTH) ** 0.25
LN_EPS = 1e-6

OFF_Q, OFF_K, OFF_V, OFF_Z, OFF_A, OFF_B = 0, 512, 1024, 1536, 2048, 2056
OFF_CONF, OFF_FFT, OFF_SGU, OFF_GATE = 2064, 3088, 3600, 4624
N_IN = OFF_GATE + N_BRANCH * D_MODEL

LANE = 128
PB_GATE = 0
PB_Q, PB_K, PB_V, PB_Z = 32, 36, 40, 44
PB_CA, PB_CG = 48, 52
PB_FFT = 56
PB_SU, PB_SV = 60, 64
P_COLS = 68 * LANE

VMEM_LIMIT_BYTES = 52 * 1024 * 1024


def _cparams(sem):
    return pltpu.CompilerParams(dimension_semantics=sem, vmem_limit_bytes=VMEM_LIMIT_BYTES)


def _sigmoid(x):
    return 1.0 / (1.0 + jnp.exp(-x))


def _silu(x):
    return x * _sigmoid(x)


def _softplus(x):
    return jnp.maximum(x, 0.0) + jnp.log1p(jnp.exp(-jnp.abs(x)))


def _ln(x):
    mu = jnp.mean(x, axis=-1, keepdims=True)
    xc = x - mu
    var = jnp.mean(xc * xc, axis=-1, keepdims=True)
    return xc * lax.rsqrt(var + LN_EPS)


def _iota(shape, dim):
    return lax.broadcasted_iota(jnp.int32, shape, dim)


def _mod_kernel(c_ref, w_ref, b_ref, o_ref):
    s = _silu(c_ref[...])
    o_ref[...] = jnp.dot(s, w_ref[...], precision=HIGHEST, preferred_element_type=F32) + b_ref[...]


def _mod_call(cvec, w_mod, b_mod):
    nl, d, n6 = w_mod.shape
    rows = cvec.shape[0]
    tn = 512
    return pl.pallas_call(
        _mod_kernel,
        grid=(nl, n6 // tn),
        in_specs=[pl.BlockSpec((rows, d), lambda l, j: (0, 0)),
                  pl.BlockSpec((None, d, tn), lambda l, j: (l, 0, j)),
                  pl.BlockSpec((None, 1, tn), lambda l, j: (l, 0, j))],
        out_specs=pl.BlockSpec((None, rows, tn), lambda l, j: (l, 0, j)),
        out_shape=jax.ShapeDtypeStruct((nl, rows, n6), F32),
        compiler_params=_cparams(("arbitrary", "arbitrary")),
        name="mod",
    )(cvec, w_mod, b_mod.reshape(nl, 1, n6))


def _inproj_kernel(h_ref, sh_ref, sc_ref, w_ref, b_ref, wab_ref, bab_ref, p_ref, ab_ref, xs_ref):
    @pl.when(pl.program_id(1) == 0)
    def _():
        xm = _ln(h_ref[...]) * (1.0 + sc_ref[...]) + sh_ref[...]
        xs_ref[...] = xm.astype(BF16)
        ab_ref[...] = jnp.dot(xm, wab_ref[...], precision=HIGHEST, preferred_element_type=F32) + bab_ref[...]

    p_ref[...] = (jnp.dot(xs_ref[...], w_ref[...], preferred_element_type=F32) + b_ref[...]).astype(p_ref.dtype)


def _inproj_call(h, row_off, n_rows, shift, scale, tiles_per_group, w, b, wab, bab, tm, tn):
    d = h.shape[1]
    ncols = w.shape[1]
    ngrp = shift.shape[0]
    gidx = lambda i, j: (jnp.minimum(i // tiles_per_group, ngrp - 1), 0, 0)
    return pl.pallas_call(
        _inproj_kernel,
        grid=(n_rows // tm, ncols // tn),
        in_specs=[pl.BlockSpec((tm, d), lambda i, j: (i + row_off // tm, 0)),
                  pl.BlockSpec((None, 1, d), gidx),
                  pl.BlockSpec((None, 1, d), gidx),
                  pl.BlockSpec((d, tn), lambda i, j: (0, j)),
                  pl.BlockSpec((1, tn), lambda i, j: (0, j)),
                  pl.BlockSpec((d, LANE), lambda i, j: (0, 0)),
                  pl.BlockSpec((1, LANE), lambda i, j: (0, 0))],
        out_specs=[pl.BlockSpec((tm, tn), lambda i, j: (i, j)),
                   pl.BlockSpec((tm, LANE), lambda i, j: (i, 0))],
        out_shape=[jax.ShapeDtypeStruct((n_rows, ncols), BF16),
                   jax.ShapeDtypeStruct((n_rows, LANE), F32)],
        scratch_shapes=[pltpu.VMEM((tm, d), BF16)],
        compiler_params=_cparams(("arbitrary", "arbitrary")),
        name="inproj",
    )(h, shift, scale, w, b, wab, bab)


def _gdn_kernel(qp_ref, kp_ref, vp_ref, zp_ref, cwq_ref, cwk_ref, cwv_ref, ab_ref, prm_ref, nw_ref, s0_ref,
                y_ref, sout_ref,
                pad_ref, qs_ref, ks_ref, vs_ref, gb_ref, mb_ref, qa_ref, egl_ref, o_ref, *, seq, with_y):
    cs = GDN_CHUNK
    nc = seq // cs
    dk = GDN_DK
    head = pl.program_id(1)

    pad_ref[pl.ds(0, 8), :] = jnp.zeros((8, dk), F32)
    pad_ref[pl.ds(8 + seq, 8), :] = jnp.zeros((8, dk), F32)

    def conv_silu(x_ref, cw_ref):
        pad_ref[pl.ds(8, seq), :] = x_ref[...].astype(F32)
        half = SHORT_CONV // 2
        acc = None
        for j in range(SHORT_CONV):
            term = cw_ref[pl.ds(j, 1), :] * pad_ref[pl.ds(8 + j - half, seq), :]
            acc = term if acc is None else acc + term
        return _silu(acc)

    def l2n(x):
        return x * lax.rsqrt(jnp.sum(x * x, axis=-1, keepdims=True) + LN_EPS)

    qs_ref[...] = l2n(conv_silu(qp_ref, cwq_ref)) * (dk ** -0.5)
    ks_ref[...] = l2n(conv_silu(kp_ref, cwk_ref))
    vs_ref[...] = conv_silu(vp_ref, cwv_ref)

    lane_t = _iota((seq, LANE), 1)
    xab = ab_ref[...]
    gval = -prm_ref[pl.ds(0, 1), :] * _softplus(xab + prm_ref[pl.ds(1, 1), :])
    gb_ref[...] = jnp.where(lane_t < 2 * GDN_HEADS, gval, _sigmoid(xab))

    row = _iota((cs, cs), 0)
    col = _iota((cs, cs), 1)
    eye = row == col
    lane_c = _iota((cs, LANE), 1)
    row_c = _iota((cs, 1), 0)

    def bdot(a, b):
        return jnp.einsum('bij,bjk->bik', a, b, preferred_element_type=F32)

    masks = []
    for direction in range(2):
        if direction == 0:
            incl, strict, last = row >= col, row > col, cs - 1
        else:
            incl, strict, last = row <= col, row < col, 0
        masks.append((incl, strict, last, jnp.where(incl, 1.0, 0.0).astype(BF16)))

    pair_masks = [((row >> (k + 1)) == (col >> (k + 1))) & ((row >> k) != (col >> k))
                  for k in range(int(math.log2(cs)))]

    chunks_per_iter = 8 if nc % 8 == 0 else (4 if nc % 4 == 0 else 1)

    def precompute_body(i, carry):
        items = []
        for c in range(chunks_per_iter):
            n = i * chunks_per_iter + c
            r0 = pl.multiple_of(n * cs, cs)
            kc = ks_ref[pl.ds(r0, cs), :]
            qc = qs_ref[pl.ds(r0, cs), :]
            vc = vs_ref[pl.ds(r0, cs), :]
            graw = gb_ref[pl.ds(r0, cs), :]
            kq = lax.dot_general(jnp.concatenate([kc, qc], axis=0).astype(BF16), kc.astype(BF16),
                                 (((1,), (1,)), ((), ())), preferred_element_type=F32)
            for direction in range(2):
                items.append(dict(n=n, r0=r0, d=direction, kc=kc, qc=qc, vc=vc, graw=graw,
                                  gram=kq[:cs], qkt=kq[cs:]))

        graws = jnp.stack([it['graw'] for it in items])
        tris = jnp.stack([masks[it['d']][3] for it in items])
        g_hi = graws.astype(BF16)
        g_r1 = graws - g_hi.astype(F32)
        g_mid = g_r1.astype(BF16)
        g_lo = (g_r1 - g_mid.astype(F32)).astype(BF16)
        gcums = bdot(tris, g_hi) + bdot(tris, g_mid) + bdot(tris, g_lo)

        for j, it in enumerate(items):
            incl, strict, last, _ = masks[it['d']]
            cg = it['d'] * GDN_HEADS + head
            cb = 2 * GDN_HEADS + it['d'] * GDN_HEADS + head
            gcol = jnp.sum(jnp.where(lane_c == cg, gcums[j], 0.0), axis=1, keepdims=True)
            bcol = jnp.sum(jnp.where(lane_c == cb, it['graw'], 0.0), axis=1, keepdims=True)
            grow = jnp.sum(jnp.where(eye, gcol, 0.0), axis=0, keepdims=True)
            decay = jnp.where(incl, jnp.exp(jnp.minimum(gcol - grow, 0.0)), 0.0)
            glast = jnp.sum(jnp.where(row_c == last, gcol, 0.0), axis=0, keepdims=True)
            egc = jnp.exp(gcol)
            it['egc'] = egc
            it['eglast'] = jnp.exp(glast)
            it['kd'] = (it['kc'] * jnp.exp(glast - gcol)).astype(BF16)
            it['lmat'] = jnp.where(strict, it['gram'] * bcol * decay, 0.0)
            it['attn'] = (it['qkt'] * decay).astype(BF16)
            it['x'] = jnp.concatenate([it['vc'] * bcol, it['kc'] * (bcol * egc)], axis=1)

        lmats = jnp.stack([it['lmat'] for it in items])
        minv = jnp.where(eye, 1.0, 0.0)[None] - jnp.where(pair_masks[0][None], lmats, 0.0)
        lmats_b = lmats.astype(BF16)
        zero_b = jnp.zeros((), BF16)
        for pm in pair_masks[1:]:
            minv_b = minv.astype(BF16)
            t = bdot(jnp.where(pm[None], lmats_b, zero_b), minv_b)
            minv = minv - bdot(minv_b, t.astype(BF16))
        x = bdot(minv.astype(BF16), jnp.stack([it['x'] for it in items]).astype(BF16))
        xb = x.astype(BF16)

        for j, it in enumerate(items):
            mb_ref[it['d'], it['n']] = lax.dot_general(it['kd'], xb[j], (((0,), (0,)), ((), ())),
                                                       preferred_element_type=F32)
            au_aw = jnp.dot(it['attn'], xb[j], preferred_element_type=F32)
            qa_ref[it['d'], pl.ds(it['r0'], cs), :] = jnp.concatenate(
                [au_aw[:, :dk], it['qc'] * it['egc'] - au_aw[:, dk:]], axis=1)
            egl_ref[it['d'], it['n']] = jnp.broadcast_to(it['eglast'], (8, LANE))
        return carry

    lax.fori_loop(0, nc // chunks_per_iter, precompute_body, 0)

    def scan_body(i, carry):
        new = []
        for direction in range(2):
            s = carry[direction]
            n = i if direction == 0 else nc - 1 - i
            r0 = pl.multiple_of(n * cs, cs)
            sb = s.astype(BF16)
            qa = qa_ref[direction, pl.ds(r0, cs), :]
            mb = mb_ref[direction, n]
            if with_y:
                o = jnp.dot(qa[:, dk:].astype(BF16), sb, preferred_element_type=F32) + qa[:, :dk]
                if direction == 0:
                    o_ref[0, pl.ds(r0, cs), :] = o
                else:
                    o_ref[1, pl.ds(r0, cs), :] = o
            egl = egl_ref[direction, n][0:1, :]
            s = egl * s - jnp.dot(mb[:, dk:].astype(BF16), sb, preferred_element_type=F32) + mb[:, :dk]
            new.append(s)
        return tuple(new)

    s_fin = lax.fori_loop(0, nc, scan_body, (s0_ref[0], s0_ref[1]))
    sout_ref[0] = s_fin[0]
    sout_ref[1] = s_fin[1]

    if with_y:
        o = o_ref[0] + o_ref[1]
        rms = o * lax.rsqrt(jnp.mean(o * o, axis=-1, keepdims=True) + LN_EPS) * nw_ref[...]
        y_ref[...] = (rms * _silu(zp_ref[...].astype(F32))).astype(y_ref.dtype)
    else:
        y_ref[...] = jnp.zeros(y_ref.shape, y_ref.dtype)


def _gdn_call(p3, ab3, boff, bsz, blk_q, blk_k, blk_v, blk_z, conv_w8, prm, norm_w, s0, with_y):
    seq = p3.shape[1]
    dk = GDN_DK
    nc = seq // GDN_CHUNK
    pblk = lambda base: pl.BlockSpec((None, seq, dk), lambda b, h: (b + boff, 0, base + h))
    cblk = lambda base: pl.BlockSpec((8, dk), lambda b, h: (0, base + h))
    ydim = seq if with_y else 8
    kern = functools.partial(_gdn_kernel, seq=seq, with_y=with_y)
    return pl.pallas_call(
        kern,
        grid=(bsz, GDN_HEADS),
        in_specs=[pblk(blk_q), pblk(blk_k), pblk(blk_v), pblk(blk_z),
                  cblk(0), cblk(GDN_HEADS), cblk(2 * GDN_HEADS),
                  pl.BlockSpec((None, seq, LANE), lambda b, h: (b + boff, 0, 0)),
                  pl.BlockSpec((8, LANE), lambda b, h: (0, 0)),
                  pl.BlockSpec((1, dk), lambda b, h: (0, 0)),
                  pl.BlockSpec((None, None, 2, dk, dk), lambda b, h: (b, h, 0, 0, 0))],
        out_specs=[pl.BlockSpec((None, ydim, dk), lambda b, h: (b, 0, h)),
                   pl.BlockSpec((None, None, 2, dk, dk), lambda b, h: (b, h, 0, 0, 0))],
        out_shape=[jax.ShapeDtypeStruct((bsz, ydim, GDN_HEADS * dk), BF16),
                   jax.ShapeDtypeStruct((bsz, GDN_HEADS, 2, dk, dk), F32)],
        scratch_shapes=[pltpu.VMEM((seq + 16, dk), F32),
                        pltpu.VMEM((seq, dk), F32), pltpu.VMEM((seq, dk), F32), pltpu.VMEM((seq, dk), F32),
                        pltpu.VMEM((seq, LANE), F32),
                        pltpu.VMEM((2, nc, dk, 2 * dk), F32),
                        pltpu.VMEM((2, seq, 2 * dk), F32),
                        pltpu.VMEM((2, nc, 8, LANE), F32),
                        pltpu.VMEM((2, seq, dk), F32)],
        compiler_params=_cparams(("arbitrary", "arbitrary")),
        name="gdn",
    )(p3, p3, p3, p3, conv_w8, conv_w8, conv_w8, ab3, prm, norm_w, s0)


def _conf_kernel(a_ref, g_ref, w_ref, b_ref, lg_ref, lb_ref, y_ref, pad_ref, *, seq):
    halo = 16
    pad_ref[pl.ds(0, halo), :] = jnp.zeros((halo, LANE), F32)
    pad_ref[pl.ds(halo + seq, halo), :] = jnp.zeros((halo, LANE), F32)
    pad_ref[pl.ds(halo, seq), :] = a_ref[...].astype(F32) * _sigmoid(g_ref[...].astype(F32))
    half = CONF_KW // 2
    acc = None
    for j in range(CONF_KW):
        term = w_ref[pl.ds(j, 1), :] * pad_ref[pl.ds(halo + j - half, seq), :]
        acc = term if acc is None else acc + term
    hcv = acc + b_ref[...]
    y_ref[...] = _silu(_ln(hcv) * lg_ref[...] + lb_ref[...]).astype(y_ref.dtype)


def _conf_call(p3, boff, bsz, w32, bias, ln_g, ln_b):
    seq = p3.shape[1]
    ng = BRANCH_W // LANE
    vec = pl.BlockSpec((1, LANE), lambda b, g: (0, g))
    return pl.pallas_call(
        functools.partial(_conf_kernel, seq=seq),
        grid=(bsz, ng),
        in_specs=[pl.BlockSpec((None, seq, LANE), lambda b, g: (b + boff, 0, PB_CA + g)),
                  pl.BlockSpec((None, seq, LANE), lambda b, g: (b + boff, 0, PB_CG + g)),
                  pl.BlockSpec((32, LANE), lambda b, g: (0, g)), vec, vec, vec],
        out_specs=pl.BlockSpec((None, seq, LANE), lambda b, g: (b, 0, g)),
        out_shape=jax.ShapeDtypeStruct((bsz, seq, BRANCH_W), BF16),
        scratch_shapes=[pltpu.VMEM((seq + 32, LANE), F32)],
        compiler_params=_cparams(("arbitrary", "arbitrary")),
        name="conf",
    )(p3, p3, w32, bias, ln_g, ln_b)


def _fft_chan_kernel(x_ref, cs_ref, o_ref):
    y = jnp.dot(x_ref[...].astype(BF16), cs_ref[...], preferred_element_type=F32)
    o_ref[0] = y[:, :LANE].astype(o_ref.dtype)
    o_ref[1] = y[:, LANE:].astype(o_ref.dtype)


def _fft_chan_call(p3, boff, bsz, cs_mat):
    seq = p3.shape[1]
    ng = BRANCH_W // LANE
    return pl.pallas_call(
        _fft_chan_kernel,
        grid=(bsz, ng),
        in_specs=[pl.BlockSpec((None, seq, LANE), lambda b, g: (b + boff, 0, PB_FFT + g)),
                  pl.BlockSpec((LANE, 2 * LANE), lambda b, g: (0, 0))],
        out_specs=pl.BlockSpec((2, seq, LANE), lambda b, g: (0, 0, b * ng + g)),
        out_shape=jax.ShapeDtypeStruct((2, seq, bsz * BRANCH_W), BF16),
        compiler_params=_cparams(("arbitrary", "arbitrary")),
        name="fft_chan",
    )(p3, cs_mat)


def _matmul_kernel(a_ref, b_ref, o_ref):
    o_ref[...] = jnp.dot(a_ref[...], b_ref[...], preferred_element_type=F32).astype(o_ref.dtype)


def _fft_pos_call(wpos, xcs2):
    m, k = wpos.shape
    n = xcs2.shape[1]
    tm = min(m, 512)
    tn = min(n, 1024)
    return pl.pallas_call(
        _matmul_kernel,
        grid=(m // tm, n // tn),
        in_specs=[pl.BlockSpec((tm, k), lambda i, j: (i, 0)),
                  pl.BlockSpec((k, tn), lambda i, j: (0, j))],
        out_specs=pl.BlockSpec((tm, tn), lambda i, j: (i, j)),
        out_shape=jax.ShapeDtypeStruct((m, n), BF16),
        compiler_params=_cparams(("arbitrary", "arbitrary")),
        name="fft_pos",
    )(wpos, xcs2)


def _gelu(x):
    return 0.5 * x * (1.0 + jnp.tanh(math.sqrt(2.0 / math.pi) * (x + 0.044715 * (x * x * x))))


def _sgu_kernel(u_ref, v_ref, lg_ref, lb_ref, ws_ref, bs_ref, y_ref, *, seq):
    ws = ws_ref[...].astype(BF16)
    bsb = bs_ref[...]
    for n in range(seq // SGU_CHUNK):
        rows = pl.ds(n * SGU_CHUNK, SGU_CHUNK)
        vn = _ln(_gelu(v_ref[rows, :].astype(F32))) * lg_ref[...] + lb_ref[...]
        s = jnp.dot(ws, vn.astype(BF16), preferred_element_type=F32) + bsb
        y_ref[rows, :] = (_gelu(u_ref[rows, :].astype(F32)) * s).astype(y_ref.dtype)


def _sgu_call(p3, boff, bsz, ln_g, ln_b, ws, bsb):
    seq = p3.shape[1]
    ng = BRANCH_W // LANE
    vec = pl.BlockSpec((1, LANE), lambda b, g: (0, g))
    return pl.pallas_call(
        functools.partial(_sgu_kernel, seq=seq),
        grid=(bsz, ng),
        in_specs=[pl.BlockSpec((None, seq, LANE), lambda b, g: (b + boff, 0, PB_SU + g)),
                  pl.BlockSpec((None, seq, LANE), lambda b, g: (b + boff, 0, PB_SV + g)),
                  vec, vec,
                  pl.BlockSpec((None, SGU_CHUNK, SGU_CHUNK), lambda b, g: (g, 0, 0)),
                  pl.BlockSpec((None, SGU_CHUNK, LANE), lambda b, g: (g, 0, 0))],
        out_specs=pl.BlockSpec((None, seq, LANE), lambda b, g: (b, 0, g)),
        out_shape=jax.ShapeDtypeStruct((bsz, seq, BRANCH_W), BF16),
        compiler_params=_cparams(("arbitrary", "arbitrary")),
        name="sgu",
    )(p3, p3, ln_g, ln_b, ws, bsb)


def _merge_kernel(yg_ref, yc_ref, yf_ref, ys_ref, gate_ref, h_ref, g1_ref, sh2_ref, sc2_ref,
                  wb_ref, wo_ref, bo_ref, lg_ref, lb_ref, wr_ref, br_ref,
                  h1_ref, xm_ref, comb_ref):
    d = D_MODEL
    acc = None
    for i, y_ref in enumerate((yg_ref, yc_ref, yf_ref, ys_ref)):
        proj = jnp.dot(y_ref[...], wb_ref[i], preferred_element_type=F32)
        term = _sigmoid(gate_ref[:, i * d:(i + 1) * d].astype(F32)) * proj
        acc = term if acc is None else acc + term
    y = jnp.dot(acc.astype(BF16), wo_ref[...], preferred_element_type=F32) + bo_ref[...]
    h1 = _ln(DN_ALPHA * h_ref[...] + g1_ref[...] * y) * lg_ref[...] + lb_ref[...]
    h1_ref[...] = h1
    xm = _ln(h1) * (1.0 + sc2_ref[...]) + sh2_ref[...]
    xm_ref[...] = xm.astype(BF16)

    logits = jnp.dot(xm, wr_ref[...], precision=HIGHEST, preferred_element_type=F32) + br_ref[...]
    lane = _iota(logits.shape, 1)
    ninf = -jnp.inf
    big = jnp.int32(1 << 20)
    gl = jnp.where(lane < MOE_GROUPS, logits, ninf)
    gmax = jnp.max(gl, axis=-1, keepdims=True)
    gsel = jnp.min(jnp.where(gl == gmax, lane, big), axis=-1, keepdims=True)
    gp = 1.0 / jnp.sum(jnp.exp(gl - gmax), axis=-1, keepdims=True)
    lo = MOE_GROUPS + gsel * MOE_PER_GROUP
    el = jnp.where((lane >= lo) & (lane < lo + MOE_PER_GROUP), logits, ninf)
    v1 = jnp.max(el, axis=-1, keepdims=True)
    i1 = jnp.min(jnp.where(el == v1, lane, big), axis=-1, keepdims=True)
    el2 = jnp.where(lane == i1, ninf, el)
    v2 = jnp.max(el2, axis=-1, keepdims=True)
    i2 = jnp.min(jnp.where(el2 == v2, lane, big), axis=-1, keepdims=True)
    e21 = jnp.exp(v2 - v1)
    w1 = gp / (1.0 + e21)
    w2 = gp * e21 / (1.0 + e21)
    comb_ref[...] = jnp.where(lane == i1, w1, 0.0) + jnp.where(lane == i2, w2, 0.0)


def _merge_call(row_off, n_rows, seq, yg, yc, yf, ys, p, h, g1, sh2, sc2, tiles_per_group,
                wb, wo, bo, lg, lb, wr, br, tm):
    d = D_MODEL
    ngrp = g1.shape[0]
    tps = seq // tm
    toff = row_off // tm
    gidx = lambda i: (jnp.minimum(i // tiles_per_group, ngrp - 1), 0, 0)
    rowblk = lambda w: pl.BlockSpec((tm, w), lambda i: (i, 0))
    const2 = lambda a: pl.BlockSpec(a.shape, lambda i: (0, 0))
    return pl.pallas_call(
        _merge_kernel,
        grid=(n_rows // tm,),
        in_specs=[rowblk(BRANCH_W), rowblk(BRANCH_W),
                  pl.BlockSpec((tm, BRANCH_W), lambda i: (i % tps, i // tps)),
                  rowblk(BRANCH_W),
                  pl.BlockSpec((tm, N_BRANCH * d), lambda i: (i + toff, PB_GATE)),
                  pl.BlockSpec((tm, d), lambda i: (i + toff, 0)),
                  pl.BlockSpec((None, 1, d), gidx), pl.BlockSpec((None, 1, d), gidx),
                  pl.BlockSpec((None, 1, d), gidx),
                  pl.BlockSpec(wb.shape, lambda i: (0, 0, 0)),
                  const2(wo), const2(bo), const2(lg), const2(lb), const2(wr), const2(br)],
        out_specs=[rowblk(d), rowblk(d), rowblk(LANE)],
        out_shape=[jax.ShapeDtypeStruct((n_rows, d), F32),
                   jax.ShapeDtypeStruct((n_rows, d), BF16),
                   jax.ShapeDtypeStruct((n_rows, LANE), F32)],
        compiler_params=_cparams(("arbitrary",)),
        name="merge",
    )(yg, yc, yf, ys, p, h, g1, sh2, sc2, wb, wo, bo, lg, lb, wr, br)


def _moe_kernel(x_ref, comb_ref, wg_ref, wu_ref, wd_ref, h1_ref, g2_ref, lg_ref, lb_ref, o_ref, acc_ref):
    e = pl.program_id(1)

    @pl.when(e == 0)
    def _():
        acc_ref[...] = jnp.zeros(acc_ref.shape, F32)

    x = x_ref[...]
    hg = jnp.dot(x, wg_ref[...].astype(BF16), preferred_element_type=F32)
    hu = jnp.dot(x, wu_ref[...].astype(BF16), preferred_element_type=F32)
    comb = comb_ref[...]
    lane = _iota(comb.shape, 1)
    ce = jnp.sum(jnp.where(lane == MOE_GROUPS + e, comb, 0.0), axis=-1, keepdims=True)
    hid = (_silu(hg) * hu * ce).astype(BF16)
    acc_ref[...] += jnp.dot(hid, wd_ref[...].astype(BF16), preferred_element_type=F32)

    @pl.when(e == N_EXPERTS - 1)
    def _():
        o_ref[...] = _ln(DN_ALPHA * h1_ref[...] + g2_ref[...] * acc_ref[...]) * lg_ref[...] + lb_ref[...]


def _moe_call(n_rows, xm, comb, wg, wu, wd, h1, g2, tiles_per_group, lg, lb, tm):
    d = D_MODEL
    ngrp = g2.shape[0]
    gidx = lambda i, e: (jnp.minimum(i // tiles_per_group, ngrp - 1), 0, 0)
    return pl.pallas_call(
        _moe_kernel,
        grid=(n_rows // tm, N_EXPERTS),
        in_specs=[pl.BlockSpec((tm, d), lambda i, e: (i, 0)),
                  pl.BlockSpec((tm, LANE), lambda i, e: (i, 0)),
                  pl.BlockSpec((None, d, EXPERT_HIDDEN), lambda i, e: (e, 0, 0)),
                  pl.BlockSpec((None, d, EXPERT_HIDDEN), lambda i, e: (e, 0, 0)),
                  pl.BlockSpec((None, EXPERT_HIDDEN, d), lambda i, e: (e, 0, 0)),
                  pl.BlockSpec((tm, d), lambda i, e: (i, 0)),
                  pl.BlockSpec((None, 1, d), gidx),
                  pl.BlockSpec((1, d), lambda i, e: (0, 0)),
                  pl.BlockSpec((1, d), lambda i, e: (0, 0))],
        out_specs=pl.BlockSpec((tm, d), lambda i, e: (i, 0)),
        out_shape=jax.ShapeDtypeStruct((n_rows, d), F32),
        scratch_shapes=[pltpu.VMEM((tm, d), F32)],
        compiler_params=_cparams(("arbitrary", "arbitrary")),
        name="moe",
    )(xm, comb, wg, wu, wd, h1, g2, lg, lb)


def _sincos_table(n, d):
    rows = n // GRID_W
    row = np.broadcast_to(np.arange(rows, dtype=np.float32)[:, None], (rows, GRID_W)).reshape(-1)
    col = np.broadcast_to(np.arange(GRID_W, dtype=np.float32)[None, :], (rows, GRID_W)).reshape(-1)
    quarter = d // 4
    omega = (1.0 / (10000.0 ** (np.arange(quarter, dtype=np.float32) / np.float32(quarter)))).astype(np.float32)

    def enc(pos):
        ang = (pos[:, None] * omega[None, :]).astype(np.float32)
        return np.concatenate([np.sin(ang), np.cos(ang)], -1)

    return np.concatenate([enc(row), enc(col)], -1).astype(np.float32)


def _dft_mats(n):
    k = np.arange(n, dtype=np.int64)
    ang = 2.0 * np.pi * ((k[:, None] * k[None, :]) % n).astype(np.float64) / n
    return np.cos(ang), np.sin(ang)


def _fft_consts(seq):
    cc, sc = _dft_mats(LANE)
    cs_mat = np.concatenate([cc, sc], axis=1)
    ct, st = _dft_mats(seq)
    wpos = np.concatenate([ct, -st], axis=1) / math.sqrt(seq * LANE)
    return jnp.asarray(cs_mat, dtype=F32).astype(BF16), jnp.asarray(wpos, dtype=F32).astype(BF16)


def _pick_tile(preferred, *dims):
    t = preferred
    while any(dim % t for dim in dims):
        t //= 2
    return t


def _to_kernel_cols(a):
    segs = [(OFF_GATE, N_IN), (OFF_Q, OFF_A), (OFF_CONF, OFF_GATE)]
    return jnp.concatenate([a[..., lo:hi] for lo, hi in segs], axis=-1)


def kernel(x, c, ctx, c_ctx, w_mod, b_mod, w_in, b_in, gdn_conv_w, gdn_a_log, gdn_dt_bias, gdn_norm_w,
           conf_dw_w, conf_dw_b, conf_ln_g, conf_ln_b, sgu_ln_g, sgu_ln_b, sgu_ws, sgu_bs,
           w_branch, w_out, b_out, ln1_g, ln1_b, ln2_g, ln2_b, router_group_w, router_group_b,
           router_expert_w, router_expert_b, expert_w_gate, expert_w_up, expert_w_down):
    bsz, seq, d = x.shape
    cseq = ctx.shape[1]
    n_lat = bsz * seq
    n_ctx = bsz * cseq
    nl = w_mod.shape[0]
    tm_in = _pick_tile(1024, seq, n_ctx)
    tm_merge = _pick_tile(256, seq, cseq)
    tm_moe = _pick_tile(1024, seq, n_ctx)

    h_all = jnp.concatenate([(x + jnp.asarray(_sincos_table(seq, d))[None]).reshape(n_lat, d),
                             ctx.reshape(n_ctx, d)], axis=0)

    cvec = jnp.concatenate([c, c_ctx[None, :], jnp.zeros((16 - bsz - 1, d), F32)], axis=0)
    mod = _mod_call(cvec, w_mod, b_mod)
    ngrp = bsz + 1

    def modvec(l, k):
        return mod[l, :ngrp, k * d:(k + 1) * d].reshape(ngrp, 1, d)

    zeros_state = jnp.zeros((bsz, GDN_HEADS, 2, GDN_DK, GDN_DK), F32)
    cs_mat_l, wpos_l = _fft_consts(seq)
    cs_mat_c, wpos_c = _fft_consts(cseq)

    for l in range(nl):
        last = l == nl - 1
        w_main = _to_kernel_cols(w_in[l]).astype(BF16)
        b_main = _to_kernel_cols(b_in[l]).reshape(1, -1)
        n_ab = OFF_CONF - OFF_A
        wab = jnp.pad(w_in[l][:, OFF_A:OFF_CONF], ((0, 0), (0, LANE - n_ab)))
        bab = jnp.pad(b_in[l][OFF_A:OFF_CONF], (0, LANE - n_ab)).reshape(1, LANE)
        shift1, scale1, gate1 = modvec(l, 0), modvec(l, 1), modvec(l, 2)
        shift2, scale2, gate2 = modvec(l, 3), modvec(l, 4), modvec(l, 5)

        conv_w8 = jnp.pad(gdn_conv_w[l], ((0, 8 - SHORT_CONV), (0, 0)))
        a_row = jnp.pad(jnp.exp(gdn_a_log[l].astype(F32)).reshape(-1), (0, LANE - 2 * GDN_HEADS))
        dt_row = jnp.pad(gdn_dt_bias[l].astype(F32).reshape(-1), (0, LANE - 2 * GDN_HEADS))
        prm = jnp.concatenate([a_row[None], dt_row[None], jnp.zeros((6, LANE), F32)], axis=0)
        norm_w = gdn_norm_w[l].reshape(1, GDN_DK)

        if not last:
            n_rows = n_lat + n_ctx
            p, ab = _inproj_call(h_all, 0, n_rows, shift1, scale1, seq // tm_in, w_main, b_main, wab, bab,
                                 tm_in, 512)
            p_ctx3 = p.reshape(n_rows // cseq, cseq, P_COLS)
            ab_ctx3 = ab.reshape(n_rows // cseq, cseq, LANE)
            ctx_off = n_lat // cseq
            blk = (PB_Q, PB_K, PB_V, PB_Z)
        else:
            n_rows = n_lat
            p, ab = _inproj_call(h_all, 0, n_rows, shift1, scale1, seq // tm_in, w_main, b_main, wab, bab,
                                 tm_in, 512)
            w_qkv = w_main[:, PB_Q * LANE:PB_Z * LANE]
            b_qkv = b_main[:, PB_Q * LANE:PB_Z * LANE]
            pc, abc = _inproj_call(h_all, n_lat, n_ctx, shift1[bsz:], scale1[bsz:], 1, w_qkv, b_qkv,
                                   wab, bab, tm_in, 512)
            p_ctx3 = pc.reshape(bsz, cseq, 3 * BRANCH_W)
            ab_ctx3 = abc.reshape(bsz, cseq, LANE)
            ctx_off = 0
            blk = (0, GDN_HEADS, 2 * GDN_HEADS, 0)
        p_lat3 = p.reshape(n_rows // seq, seq, P_COLS)
        ab_lat3 = ab.reshape(n_rows // seq, seq, LANE)

        yg_ctx, s_ctx = _gdn_call(p_ctx3, ab_ctx3, ctx_off, bsz, *blk, conv_w8, prm, norm_w, zeros_state,
                                  not last)
        yg_lat, _ = _gdn_call(p_lat3, ab_lat3, 0, bsz, PB_Q, PB_K, PB_V, PB_Z, conv_w8, prm, norm_w,
                              s_ctx, True)

        conf_w32 = jnp.pad(conf_dw_w[l], ((0, 32 - CONF_KW), (0, 0)))
        conf_args = (conf_w32, conf_dw_b[l].reshape(1, -1), conf_ln_g[l].reshape(1, -1),
                     conf_ln_b[l].reshape(1, -1))
        sgu_args = (sgu_ln_g[l].reshape(1, -1), sgu_ln_b[l].reshape(1, -1), sgu_ws[l],
                    jnp.broadcast_to(sgu_bs[l][:, :, None], (BRANCH_W // LANE, SGU_CHUNK, LANE)))

        def seq_branches(p3, boff, cs_mat, wpos):
            t_ = p3.shape[1]
            yc = _conf_call(p3, boff, bsz, *conf_args).reshape(bsz * t_, BRANCH_W)
            xcs = _fft_chan_call(p3, boff, bsz, cs_mat).reshape(2 * t_, bsz * BRANCH_W)
            yf = _fft_pos_call(wpos, xcs)
            ys = _sgu_call(p3, boff, bsz, *sgu_args).reshape(bsz * t_, BRANCH_W)
            return yc, yf, ys

        wb = w_branch[l].astype(BF16)
        wo = w_out[l].astype(BF16)
        wr = jnp.pad(jnp.concatenate([router_group_w[l], router_expert_w[l]], axis=1),
                     ((0, 0), (0, LANE - MOE_GROUPS - N_EXPERTS)))
        br = jnp.pad(jnp.concatenate([router_group_b[l], router_expert_b[l]]),
                     (0, LANE - MOE_GROUPS - N_EXPERTS)).reshape(1, LANE)
        merge_w = (wb, wo, b_out[l].reshape(1, d), ln1_g[l].reshape(1, d), ln1_b[l].reshape(1, d), wr, br)
        moe_w = (expert_w_gate[l], expert_w_up[l], expert_w_down[l])
        ln2 = (ln2_g[l].reshape(1, d), ln2_b[l].reshape(1, d))

        def channel_mix(row_off, rows, seq_len, yg, yc, yf, ys, g1, sh2, sc2, g2, tpg_merge, tpg_moe):
            h1, xm, comb = _merge_call(row_off, rows, seq_len, yg, yc, yf, ys, p, h_all, g1, sh2, sc2,
                                       tpg_merge, *merge_w, tm_merge)
            return _moe_call(rows, xm, comb, *moe_w, h1, g2, tpg_moe, *ln2, tm_moe)

        yc, yf, ys = seq_branches(p_lat3, 0, cs_mat_l, wpos_l)
        h_lat = channel_mix(0, n_lat, seq, yg_lat.reshape(n_lat, BRANCH_W), yc, yf, ys,
                            gate1, shift2, scale2, gate2, seq // tm_merge, seq // tm_moe)
        if last:
            return h_lat.reshape(bsz, seq, d)

        ycc, yfc, ysc = seq_branches(p_ctx3, ctx_off, cs_mat_c, wpos_c)
        h_ctx = channel_mix(n_lat, n_ctx, cseq, yg_ctx.reshape(n_ctx, BRANCH_W), ycc, yfc, ysc,
                            gate1[bsz:], shift2[bsz:], scale2[bsz:], gate2[bsz:], 1, 1)
        h_all = jnp.concatenate([h_lat, h_ctx], axis=0)
```
